```python
import jax, jax.numpy as jnp
from jax import lax
import numpy as np

D_MODEL = 1024
BATCH = 16
SEQ = 4096
DEPTH = 1

CTX_LEN = 256
GRID_W = 64
D_MIX = D_MODEL
GLA_HEADS = 4
GLA_DK = 64
GLA_DV = 128
GLA_KEY = GLA_HEADS * GLA_DK
GLA_VAL = GLA_HEADS * GLA_DV
QKV_W = 2 * GLA_KEY + GLA_VAL
GATE_RANK = 16
GATE_NORMALIZER = 16.0
GLA_CHUNK = 64
CONV_K = 3
CMLP_HEADS = 8
CMLP_WIDTH = D_MIX - GLA_VAL
CMLP_CHUNK = 128
D_IN = QKV_W + 2 * GATE_RANK + GLA_VAL + 2 * CMLP_WIDTH
D_FF = 2816
ADA_MODS = 9
EPS = 1e-6

kernel_name = "hybrid_gla_gmlp_macaron_dit_block"


def rmsnorm(x, g):
    xf = x.astype(jnp.float32)
    n = xf * lax.rsqrt(jnp.mean(xf * xf, axis=-1, keepdims=True) + EPS)
    return (n * g.astype(jnp.float32)).astype(x.dtype)


def modulate(x, shift, scale):
    return x * (1 + scale) + shift


def adaln(cond, w, b):
    return (jax.nn.silu(cond) @ w + b).reshape(cond.shape[0], ADA_MODS, 1, D_MODEL)


def swiglu(y, w_in, w_out):
    gate, up = jnp.split(y @ w_in, 2, axis=-1)
    return (jax.nn.silu(gate) * up) @ w_out


def grid_dwconv(z, w, rows, cols):
    B, T, C = z.shape
    img = z.reshape(B, rows, cols, C)
    y = lax.conv_general_dilated(img, w[:, :, None, :].astype(z.dtype), (1, 1), 'SAME',
                                 dimension_numbers=('NHWC', 'HWIO', 'NHWC'),
                                 feature_group_count=C)
    return y.reshape(B, T, C)


def log_decay(z, w, b):
    B, T = z.shape[:2]
    a = jax.nn.log_sigmoid((z @ w + b).astype(jnp.float32)) / GATE_NORMALIZER
    return a.reshape(B, T, GLA_HEADS, GLA_DK)


def gla_scan(q, k, v, g, s0):
    B, T, H, _ = q.shape
    DV = v.shape[-1]
    nc = T // GLA_CHUNK

    def to_chunks(a):
        return a.reshape(B, nc, GLA_CHUNK, H, a.shape[-1]).transpose(1, 0, 3, 2, 4)

    lower = jnp.tril(jnp.ones((GLA_CHUNK, GLA_CHUNK), dtype=bool))[:, :, None]

    def step(s, xs):
        qc, kc, vc, gc = xs
        b = jnp.cumsum(gc, axis=2)
        rel = b[:, :, :, None, :] - b[:, :, None, :, :]
        decay = jnp.exp(jnp.where(lower, rel, -jnp.inf))
        scores = jnp.einsum('bhid,bhjd,bhijd->bhij', qc, kc, decay)
        o = (jnp.einsum('bhij,bhjv->bhiv', scores, vc)
             + jnp.einsum('bhid,bhdv->bhiv', qc * jnp.exp(b), s))
        b_last = b[:, :, -1:, :]
        s = (jnp.exp(b_last)[:, :, 0, :, None] * s
             + jnp.einsum('bhjd,bhjv->bhdv', kc * jnp.exp(b_last - b), vc))
        return s, o

    s, o = lax.scan(step, s0, (to_chunks(q), to_chunks(k), to_chunks(v), to_chunks(g)))
    o = o.transpose(1, 0, 3, 2, 4).reshape(B, T, H, DV)
    return o.astype(v.dtype), s


def gla_bidirectional(ctx_in, lat_in):
    qc, kc, vc, gfc, gbc = ctx_in
    qx, kx, vx, gfx, gbx = lat_in
    B = qc.shape[0]
    flip = lambda a: a[:, ::-1]
    s0 = jnp.zeros((B, GLA_HEADS, GLA_DK, GLA_DV), jnp.float32)
    o_cf, s_f = gla_scan(qc, kc, vc, gfc, s0)
    o_cb, s_b = gla_scan(flip(qc), flip(kc), flip(vc), flip(gbc), s0)
    o_xf, _ = gla_scan(qx, kx, vx, gfx, s_f)
    o_xb, _ = gla_scan(flip(qx), flip(kx), flip(vx), flip(gbx), s_b)
    return o_cf + flip(o_cb), o_xf + flip(o_xb)


def mixer_inputs(y, w_in, conv_w, w_gf, b_gf, w_gb, b_gb, rows, cols):
    B, T = y.shape[:2]
    z = y @ w_in
    i1 = QKV_W
    i2 = i1 + GATE_RANK
    i3 = i2 + GATE_RANK
    i4 = i3 + GLA_VAL
    i5 = i4 + CMLP_WIDTH
    qkv, glr_f, glr_b, og, u, vs = jnp.split(z, [i1, i2, i3, i4, i5], axis=-1)
    qkv = jax.nn.silu(grid_dwconv(qkv, conv_w, rows, cols))
    q, k, v = jnp.split(qkv, [GLA_KEY, 2 * GLA_KEY], axis=-1)
    q = q.reshape(B, T, GLA_HEADS, GLA_DK) * (GLA_DK ** -0.5)
    k = k.reshape(B, T, GLA_HEADS, GLA_DK)
    v = v.reshape(B, T, GLA_HEADS, GLA_DV)
    gla_in = (q, k, v, log_decay(glr_f, w_gf, b_gf), log_decay(glr_b, w_gb, b_gb))
    return gla_in, og, u, vs


def chunk_mlp(u, vs, norm_g, w_s, b_s):
    B, T, C = vs.shape
    u = jax.nn.gelu(u)
    vs = rmsnorm(jax.nn.gelu(vs), norm_g)
    vh = vs.reshape(B, T // CMLP_CHUNK, CMLP_CHUNK, CMLP_HEADS, C // CMLP_HEADS)
    mixed = jnp.einsum('hij,bnjhd->bnihd', w_s, vh) + b_s.T[None, None, :, :, None]
    return u * mixed.reshape(B, T, C)


def merge_out(o_gla, og, u, vs, gla_g, cm_g, w_s, b_s, w_out):
    B, T = og.shape[:2]
    a = rmsnorm(o_gla, gla_g).reshape(B, T, GLA_VAL) * jax.nn.silu(og)
    bm = chunk_mlp(u, vs, cm_g, w_s, b_s)
    return jnp.concatenate([a, bm], axis=-1) @ w_out


def setup_inputs(seed: int = 0) -> dict:
    key = jax.random.key(seed)
    ks = jax.random.split(key, 32)
    f32 = jnp.float32
    L, D = DEPTH, D_MODEL

    def nrm(k, shape, scale):
        return jax.random.normal(k, shape, f32) * scale

    return {
        "x": nrm(ks[0], (BATCH, SEQ, D), 1.0),
        "c": nrm(ks[1], (BATCH, D), 1.0),
        "ctx": nrm(ks[2], (BATCH, CTX_LEN, D), 1.0),
        "c_ctx": nrm(ks[3], (D,), 1.0),
        "w_ada": nrm(ks[4], (L, D, ADA_MODS * D), 0.5 * D ** -0.5),
        "b_ada": nrm(ks[5], (L, ADA_MODS * D), 0.02),
        "norm1_g": 1.0 + nrm(ks[6], (L, D), 0.02),
        "ff1_in": nrm(ks[7], (L, D, 2 * D_FF), D ** -0.5),
        "ff1_out": nrm(ks[8], (L, D_FF, D), D_FF ** -0.5),
        "norm2_g": 1.0 + nrm(ks[9], (L, D), 0.02),
        "w_in": nrm(ks[10], (L, D, D_IN), D ** -0.5),
        "conv_w": nrm(ks[11], (L, CONV_K, CONV_K, QKV_W), 1.0 / CONV_K),
        "w_gate_f": nrm(ks[12], (L, GATE_RANK, GLA_KEY), GATE_RANK ** -0.5),
        "b_gate_f": nrm(ks[13], (L, GLA_KEY), 0.1),
        "w_gate_b": nrm(ks[14], (L, GATE_RANK, GLA_KEY), GATE_RANK ** -0.5),
        "b_gate_b": nrm(ks[15], (L, GLA_KEY), 0.1),
        "gla_norm_g": 1.0 + nrm(ks[16], (L, GLA_DV), 0.02),
        "cmlp_norm_g": 1.0 + nrm(ks[17], (L, CMLP_WIDTH), 0.02),
        "w_s": nrm(ks[18], (L, CMLP_HEADS, CMLP_CHUNK, CMLP_CHUNK), CMLP_CHUNK ** -0.5),
        "b_s": 1.0 + nrm(ks[19], (L, CMLP_HEADS, CMLP_CHUNK), 0.02),
        "w_out": nrm(ks[20], (L, D_MIX, D), D_MIX ** -0.5),
        "norm3_g": 1.0 + nrm(ks[21], (L, D), 0.02),
        "ff2_in": nrm(ks[22], (L, D, 2 * D_FF), D ** -0.5),
        "ff2_out": nrm(ks[23], (L, D_FF, D), D_FF ** -0.5),
        "final_g": 1.0 + nrm(ks[24], (D,), 0.02),
    }


def reference(x, c, ctx, c_ctx, w_ada, b_ada, norm1_g, ff1_in, ff1_out, norm2_g, w_in,
              conv_w, w_gate_f, b_gate_f, w_gate_b, b_gate_b, gla_norm_g, cmlp_norm_g,
              w_s, b_s, w_out, norm3_g, ff2_in, ff2_out, final_g):
    rows = x.shape[1] // GRID_W
    h = ctx
    for l in range(DEPTH):
        last = l == DEPTH - 1
        m_x = adaln(c, w_ada[l], b_ada[l])
        m_c = adaln(c_ctx[None], w_ada[l], b_ada[l])

        x = x + 0.5 * m_x[:, 2] * swiglu(
            modulate(rmsnorm(x, norm1_g[l]), m_x[:, 0], m_x[:, 1]), ff1_in[l], ff1_out[l])
        h = h + 0.5 * m_c[:, 2] * swiglu(
            modulate(rmsnorm(h, norm1_g[l]), m_c[:, 0], m_c[:, 1]), ff1_in[l], ff1_out[l])

        y_x = modulate(rmsnorm(x, norm2_g[l]), m_x[:, 3], m_x[:, 4])
        y_c = modulate(rmsnorm(h, norm2_g[l]), m_c[:, 3], m_c[:, 4])
        gla_x, og_x, u_x, vs_x = mixer_inputs(y_x, w_in[l], conv_w[l], w_gate_f[l], b_gate_f[l],
                                              w_gate_b[l], b_gate_b[l], rows, GRID_W)
        gla_c, og_c, u_c, vs_c = mixer_inputs(y_c, w_in[l], conv_w[l], w_gate_f[l], b_gate_f[l],
                                              w_gate_b[l], b_gate_b[l], 1, CTX_LEN)
        o_c, o_x = gla_bidirectional(gla_c, gla_x)
        mix_x = merge_out(o_x, og_x, u_x, vs_x, gla_norm_g[l], cmlp_norm_g[l],
                          w_s[l], b_s[l], w_out[l])
        x = x + m_x[:, 5] * mix_x

        x = x + 0.5 * m_x[:, 8] * swiglu(
            modulate(rmsnorm(x, norm3_g[l]), m_x[:, 6], m_x[:, 7]), ff2_in[l], ff2_out[l])

        if not last:
            mix_c = merge_out(o_c, og_c, u_c, vs_c, gla_norm_g[l], cmlp_norm_g[l],
                              w_s[l], b_s[l], w_out[l])
            h = h + m_c[:, 5] * mix_c
            h = h + 0.5 * m_c[:, 8] * swiglu(
                modulate(rmsnorm(h, norm3_g[l]), m_c[:, 6], m_c[:, 7]), ff2_in[l], ff2_out[l])
    return rmsnorm(x, final_g)
```

```python
import functools

import jax
import jax.numpy as jnp
import numpy as np
from jax import lax
from jax.experimental import pallas as pl
from jax.experimental.pallas import tpu as pltpu

F32 = jnp.float32
BF16 = jnp.bfloat16

D_MODEL = 1024
CTX_LEN = 256
GRID_W = 64
GLA_HEADS = 4
GLA_DK = 64
GLA_DV = 128
GLA_KEY = GLA_HEADS * GLA_DK
GLA_VAL = GLA_HEADS * GLA_DV
QKV_W = 2 * GLA_KEY + GLA_VAL
GATE_RANK = 16
GATE_NORMALIZER = 16.0
GLA_CHUNK = 64
CMLP_HEADS = 8
CMLP_WIDTH = 512
CMLP_CHUNK = 128
D_FF = 2816
ADA_MODS = 9
EPS = 1e-6

FF_COLS = 256
FF_CHUNKS = D_FF // FF_COLS
GLR_PAD = 128
REST_W = GLA_VAL + 2 * CMLP_WIDTH
VMEM_LIMIT = 56 * 1024 * 1024


def _cparams(*sem):
    return pltpu.CompilerParams(dimension_semantics=sem, vmem_limit_bytes=VMEM_LIMIT)


def _rms(x, g):
    ms = jnp.mean(x * x, axis=-1, keepdims=True)
    return x * lax.rsqrt(ms + EPS) * g


def _modnorm(x, g, shift, scale):
    return _rms(x, g) * (1.0 + scale) + shift


def _ada_kernel(cond_ref, w_ref, b_ref, o_ref):
    c = cond_ref[...]
    s = (c * jax.nn.sigmoid(c)).astype(BF16)
    o_ref[...] = jnp.dot(s, w_ref[...].astype(BF16), preferred_element_type=F32) + b_ref[...]


def _adaln(cond, w, b):
    m, d = cond.shape
    n = w.shape[1]
    tn = D_MODEL
    return pl.pallas_call(
        _ada_kernel,
        grid=(n // tn,),
        in_specs=[
            pl.BlockSpec((m, d), lambda j: (0, 0)),
            pl.BlockSpec((d, tn), lambda j: (0, j)),
            pl.BlockSpec((1, tn), lambda j: (0, j)),
        ],
        out_specs=pl.BlockSpec((m, tn), lambda j: (0, j)),
        out_shape=jax.ShapeDtypeStruct((m, n), F32),
        compiler_params=_cparams("arbitrary"),
        name="adaln",
    )(cond, w, b.reshape(1, n))


def _ffn_kernel(x_ref, mods_ref, g_ref, win_ref, wout_ref, *rest, i_shift, i_scale, i_gate, final):
    if final:
        fg_ref, o_ref = rest
    else:
        (o_ref,) = rest
    x = x_ref[0]
    m = mods_ref[0]
    y = _modnorm(x, g_ref[...], m[i_shift:i_shift + 1], m[i_scale:i_scale + 1]).astype(BF16)
    acc = jnp.zeros(x.shape, F32)
    for c in range(FF_CHUNKS):
        h = jnp.dot(y, win_ref[c], preferred_element_type=F32)
        gate = h[:, :FF_COLS]
        up = h[:, FF_COLS:]
        act = (gate * jax.nn.sigmoid(gate) * up).astype(BF16)
        acc = acc + jnp.dot(act, wout_ref[c], preferred_element_type=F32)
    xo = x + 0.5 * m[i_gate:i_gate + 1] * acc
    if final:
        xo = _rms(xo, fg_ref[...])
    o_ref[0] = xo


def _prep_ffn(w_in, w_out):
    d = w_in.shape[0]
    wg = w_in[:, :D_FF].reshape(d, FF_CHUNKS, FF_COLS)
    wu = w_in[:, D_FF:].reshape(d, FF_CHUNKS, FF_COLS)
    win_r = jnp.concatenate([wg, wu], axis=-1).transpose(1, 0, 2).astype(BF16)
    wout_r = w_out.reshape(FF_CHUNKS, FF_COLS, d).astype(BF16)
    return win_r, wout_r


def _ffn(x, mods, norm_g, win_r, wout_r, idx, final_g=None, tm=512):
    b, t, d = x.shape
    final = final_g is not None
    in_specs = [
        pl.BlockSpec((1, tm, d), lambda i, j: (i, j, 0)),
        pl.BlockSpec((1, ADA_MODS, d), lambda i, j: (i, 0, 0)),
        pl.BlockSpec((1, d), lambda i, j: (0, 0)),
        pl.BlockSpec(win_r.shape, lambda i, j: (0, 0, 0), pipeline_mode=pl.Buffered(1)),
        pl.BlockSpec(wout_r.shape, lambda i, j: (0, 0, 0), pipeline_mode=pl.Buffered(1)),
    ]
    args = [x, mods, norm_g.reshape(1, d), win_r, wout_r]
    if final:
        in_specs.append(pl.BlockSpec((1, d), lambda i, j: (0, 0)))
        args.append(final_g.reshape(1, d))
    kern = functools.partial(_ffn_kernel, i_shift=idx[0], i_scale=idx[1], i_gate=idx[2], final=final)
    return pl.pallas_call(
        kern,
        grid=(b, t // tm),
        in_specs=in_specs,
        out_specs=pl.BlockSpec((1, tm, d), lambda i, j: (i, j, 0)),
        out_shape=jax.ShapeDtypeStruct((b, t, d), F32),
        compiler_params=_cparams("parallel", "arbitrary"),
        name="ffn_final" if final else "ffn",
    )(*args)


def _inproj_kernel(x_ref, mods_ref, g_ref, wq_ref, wl_ref, *rest, need_rest):
    if need_rest:
        wr_ref, zq_ref, zl_ref, zr_ref = rest
    else:
        zq_ref, zl_ref = rest
    m = mods_ref[0]
    y = _modnorm(x_ref[0], g_ref[...], m[3:4], m[4:5]).astype(BF16)
    zq_ref[0] = jnp.dot(y, wq_ref[...], preferred_element_type=F32)
    zl_ref[0] = jnp.dot(y, wl_ref[...], preferred_element_type=F32)
    if need_rest:
        zr_ref[0] = jnp.dot(y, wr_ref[...], preferred_element_type=F32).astype(BF16)


def _inproj(x, mods, norm_g, wq, wl, wr, need_rest, tm=512):
    b, t, d = x.shape
    const = lambda i, j: (0, 0)
    in_specs = [
        pl.BlockSpec((1, tm, d), lambda i, j: (i, j, 0)),
        pl.BlockSpec((1, ADA_MODS, d), lambda i, j: (i, 0, 0)),
        pl.BlockSpec((1, d), const),
        pl.BlockSpec(wq.shape, const, pipeline_mode=pl.Buffered(1)),
        pl.BlockSpec(wl.shape, const, pipeline_mode=pl.Buffered(1)),
    ]
    args = [x, mods, norm_g.reshape(1, d), wq, wl]
    out_specs = [
        pl.BlockSpec((1, tm, QKV_W), lambda i, j: (i, j, 0)),
        pl.BlockSpec((1, tm, GLR_PAD), lambda i, j: (i, j, 0)),
    ]
    out_shape = [
        jax.ShapeDtypeStruct((b, t, QKV_W), F32),
        jax.ShapeDtypeStruct((b, t, GLR_PAD), F32),
    ]
    if need_rest:
        in_specs.append(pl.BlockSpec(wr.shape, const, pipeline_mode=pl.Buffered(1)))
        args.append(wr)
        out_specs.append(pl.BlockSpec((1, tm, REST_W), lambda i, j: (i, j, 0)))
        out_shape.append(jax.ShapeDtypeStruct((b, t, REST_W), BF16))
    return pl.pallas_call(
        functools.partial(_inproj_kernel, need_rest=need_rest),
        grid=(b, t // tm),
        in_specs=in_specs,
        out_specs=out_specs,
        out_shape=out_shape,
        compiler_params=_cparams("parallel", "arbitrary"),
        name="inproj" if need_rest else "inproj_ctx",
    )(*args)


def _split3(x):
    h1 = x.astype(BF16)
    r1 = x - h1.astype(F32)
    h2 = r1.astype(BF16)
    h3 = (r1 - h2.astype(F32)).astype(BF16)
    return h1, h2, h3


def _conv_kernel(zc_ref, zu_ref, zd_ref, cw_ref, glr_ref, wgate_ref, bgate_ref, tri_ref,
                 qkv_ref, bf_ref, bb_ref, *, grid_w, tm, hw, lanes):
    t = pl.program_id(1)
    nt = pl.num_programs(1)
    rows_e = tm + 2 * hw
    has_up = (t > 0).astype(F32)
    has_dn = (t < nt - 1).astype(F32)
    col = lax.broadcasted_iota(jnp.int32, (rows_e, lanes), 0) % grid_w
    not_first = (col != 0).astype(F32)
    not_last = (col != grid_w - 1).astype(F32)
    for c0 in range(0, QKV_W, lanes):
        cs = slice(c0, c0 + lanes)
        e = jnp.concatenate([zu_ref[0, :, cs] * has_up, zc_ref[0, :, cs], zd_ref[0, :, cs] * has_dn], axis=0)
        el = pltpu.roll(e, 1, axis=0) * not_first
        er = pltpu.roll(e, rows_e - 1, axis=0) * not_last
        acc = jnp.zeros((tm, lanes), F32)
        for kh in range(3):
            rs = slice(kh * hw, kh * hw + tm)
            acc = acc + cw_ref[3 * kh:3 * kh + 1, cs] * el[rs]
            acc = acc + cw_ref[3 * kh + 1:3 * kh + 2, cs] * e[rs]
            acc = acc + cw_ref[3 * kh + 2:3 * kh + 3, cs] * er[rs]
        y = acc * jax.nn.sigmoid(acc)
        if c0 < GLA_KEY:
            y = y * (GLA_DK ** -0.5)
        qkv_ref[0, :, cs] = y.astype(BF16)

    z = jnp.dot(glr_ref[0], wgate_ref[...], preferred_element_type=F32,
                precision=lax.Precision.HIGHEST) + bgate_ref[...]
    g = (jnp.minimum(z, 0.0) - jnp.log1p(jnp.exp(-jnp.abs(z)))) * (1.0 / GATE_NORMALIZER)
    gf1, gf2, gf3 = _split3(g[:, :GLA_KEY])
    gb1, gb2, gb3 = _split3(g[:, GLA_KEY:])
    blk = tri_ref.shape[-1]
    lower = tri_ref[0]
    upper = tri_ref[1]
    for r0 in range(0, tm, blk):
        rs = slice(r0, r0 + blk)
        dot = lambda a, p: jnp.dot(a, p[rs], preferred_element_type=F32)
        bf_ref[0, rs, :] = dot(lower, gf1) + (dot(lower, gf2) + dot(lower, gf3))
        bb_ref[0, rs, :] = dot(upper, gb1) + (dot(upper, gb2) + dot(upper, gb3))


def _tri_blocks(blk):
    i = np.arange(blk)
    same = (i[:, None] // GLA_CHUNK) == (i[None, :] // GLA_CHUNK)
    lower = same & (i[None, :] <= i[:, None])
    upper = same & (i[None, :] >= i[:, None])
    return jnp.asarray(np.stack([lower, upper]).astype(np.float32), dtype=BF16)


def _conv_gates(zq, zl, conv_w9, wgate, bgate, grid_w, tm):
    b, t, _ = zq.shape
    hw = grid_w
    nh = tm // hw
    n_hblk = t // hw
    blk = min(tm, 256)
    tri = _tri_blocks(blk)
    const = lambda i, j: (0, 0)
    kern = functools.partial(_conv_kernel, grid_w=grid_w, tm=tm, hw=hw, lanes=256)
    return pl.pallas_call(
        kern,
        grid=(b, t // tm),
        in_specs=[
            pl.BlockSpec((1, tm, QKV_W), lambda i, j: (i, j, 0)),
            pl.BlockSpec((1, hw, QKV_W), lambda i, j: (i, jnp.maximum(j * nh - 1, 0), 0)),
            pl.BlockSpec((1, hw, QKV_W), lambda i, j: (i, jnp.minimum((j + 1) * nh, n_hblk - 1), 0)),
            pl.BlockSpec((9, QKV_W), const),
            pl.BlockSpec((1, tm, GLR_PAD), lambda i, j: (i, j, 0)),
            pl.BlockSpec((GLR_PAD, 2 * GLA_KEY), const),
            pl.BlockSpec((1, 2 * GLA_KEY), const),
            pl.BlockSpec((2, blk, blk), lambda i, j: (0, 0, 0)),
        ],
        out_specs=[
            pl.BlockSpec((1, tm, QKV_W), lambda i, j: (i, j, 0)),
            pl.BlockSpec((1, tm, GLA_KEY), lambda i, j: (i, j, 0)),
            pl.BlockSpec((1, tm, GLA_KEY), lambda i, j: (i, j, 0)),
        ],
        out_shape=[
            jax.ShapeDtypeStruct((b, t, QKV_W), BF16),
            jax.ShapeDtypeStruct((b, t, GLA_KEY), F32),
            jax.ShapeDtypeStruct((b, t, GLA_KEY), F32),
        ],
        compiler_params=_cparams("parallel", "arbitrary"),
        name="conv_gates" if grid_w == GRID_W else "conv_gates_ctx",
    )(zq, zq, zq, conv_w9, zl, wgate, bgate, tri)


def _gla_kernel(qf_ref, qb_ref, bf_ref, bb_ref, s0_ref, *rest, tg, emit_o):
    if emit_o:
        of_ref, ob_ref, st_scr = rest
    else:
        sout_ref, st_scr = rest
    t = pl.program_id(1)
    nt = pl.num_programs(1)

    @pl.when(t == 0)
    def _():
        st_scr[...] = s0_ref[0]

    nch = tg // GLA_CHUNK
    hl = GLA_HEADS * GLA_CHUNK
    r = lax.broadcasted_iota(jnp.int32, (hl, GLA_KEY), 0)
    c = lax.broadcasted_iota(jnp.int32, (hl, GLA_KEY), 1)
    same_head = (r // GLA_CHUNK) == (c // GLA_DK)
    ri = r % GLA_CHUNK
    cj = c % GLA_CHUNK
    mid = GLA_CHUNK // 2 - 1

    def stack(a):
        return jnp.where(same_head, jnp.concatenate([a] * GLA_HEADS, axis=0), jnp.zeros((), a.dtype))

    dirs = (
        (0, qf_ref, bf_ref, of_ref if emit_o else None, same_head & (cj <= ri), GLA_CHUNK - 1, range(nch)),
        (1, qb_ref, bb_ref, ob_ref if emit_o else None, same_head & (cj >= ri), 0, range(nch - 1, -1, -1)),
    )
    for d, q_ref, b_ref, o_ref, mask, last, order in dirs:
        st = st_scr[d]
        for ci in order:
            rows = slice(ci * GLA_CHUNK, (ci + 1) * GLA_CHUNK)
            q = q_ref[0, rows, 0:GLA_KEY].astype(F32)
            k = q_ref[0, rows, GLA_KEY:2 * GLA_KEY].astype(F32)
            v = q_ref[0, rows, 2 * GLA_KEY:]
            b = b_ref[0, rows, :]
            b_mid = b[mid:mid + 1]
            b_last = b[last:last + 1]
            qt = (q * jnp.exp(b - b_mid)).astype(BF16)
            kt = (k * jnp.exp(b_mid - b)).astype(BF16)
            qe = (q * jnp.exp(b)).astype(BF16)
            kh = (k * jnp.exp(b_last - b)).astype(BF16)
            k_tile = jnp.concatenate([kt] * GLA_HEADS, axis=0)
            v_stack = jnp.concatenate([v[:, h * GLA_DV:(h + 1) * GLA_DV] for h in range(GLA_HEADS)], axis=0)
            sc = lax.dot_general(stack(qt), k_tile, (((1,), (1,)), ((), ())), preferred_element_type=F32)
            sc = jnp.where(mask, sc, 0.0).astype(BF16)
            o = jnp.dot(sc, v_stack, preferred_element_type=F32)
            o = o + lax.dot_general(stack(qe), st.astype(BF16), (((1,), (1,)), ((), ())),
                                    preferred_element_type=F32)
            if emit_o:
                for h in range(GLA_HEADS):
                    o_ref[0, rows, h * GLA_DV:(h + 1) * GLA_DV] = o[h * GLA_CHUNK:(h + 1) * GLA_CHUNK]
            upd = lax.dot_general(v_stack, stack(kh), (((0,), (0,)), ((), ())), preferred_element_type=F32)
            st = st * jnp.exp(b_last) + upd
        st_scr[d] = st

    if not emit_o:
        @pl.when(t == nt - 1)
        def _():
            sout_ref[0] = st_scr[...]


def _gla(qkv, bf, bb, s0, emit_o, tg=256):
    b, t, _ = qkv.shape
    nt = t // tg
    fwd = lambda i, j: (i, j, 0)
    bwd = lambda i, j: (i, nt - 1 - j, 0)
    st_spec = pl.BlockSpec((1, 2, GLA_DV, GLA_KEY), lambda i, j: (i, 0, 0, 0))
    in_specs = [
        pl.BlockSpec((1, tg, QKV_W), fwd),
        pl.BlockSpec((1, tg, QKV_W), bwd),
        pl.BlockSpec((1, tg, GLA_KEY), fwd),
        pl.BlockSpec((1, tg, GLA_KEY), bwd),
        st_spec,
    ]
    if emit_o:
        out_specs = [pl.BlockSpec((1, tg, GLA_VAL), fwd), pl.BlockSpec((1, tg, GLA_VAL), bwd)]
        out_shape = [jax.ShapeDtypeStruct((b, t, GLA_VAL), F32)] * 2
    else:
        out_specs = st_spec
        out_shape = jax.ShapeDtypeStruct((b, 2, GLA_DV, GLA_KEY), F32)
    return pl.pallas_call(
        functools.partial(_gla_kernel, tg=tg, emit_o=emit_o),
        grid=(b, nt),
        in_specs=in_specs,
        out_specs=out_specs,
        out_shape=out_shape,
        scratch_shapes=[pltpu.VMEM((2, GLA_DV, GLA_KEY), F32)],
        compiler_params=_cparams("parallel", "arbitrary"),
        name="gla" if emit_o else "gla_ctx",
    )(qkv, qkv, bf, bb, s0)


def _merge_kernel(of_ref, ob_ref, zr_ref, x_ref, mods_ref, glag_ref, cmg_ref, ws_ref, bs_ref, wout_ref,
                  o_ref, *, tm):
    o = of_ref[0] + ob_ref[0]
    og = zr_ref[0, :, 0:GLA_VAL].astype(F32)
    parts = []
    for h in range(GLA_HEADS):
        parts.append(_rms(o[:, h * GLA_DV:(h + 1) * GLA_DV], glag_ref[...]))
    a = jnp.concatenate(parts, axis=-1) * (og * jax.nn.sigmoid(og))

    u = jax.nn.gelu(zr_ref[0, :, GLA_VAL:GLA_VAL + CMLP_WIDTH].astype(F32))
    vs = jax.nn.gelu(zr_ref[0, :, GLA_VAL + CMLP_WIDTH:].astype(F32))
    vn = _rms(vs, cmg_ref[...]).astype(BF16)
    dh = CMLP_WIDTH // CMLP_HEADS
    low = lax.broadcasted_iota(jnp.int32, (CMLP_CHUNK, 2 * dh), 1) < dh
    zero = jnp.zeros((), BF16)
    chunks = []
    for c0 in range(0, tm, CMLP_CHUNK):
        cols = []
        for p in range(CMLP_HEADS // 2):
            vp = vn[c0:c0 + CMLP_CHUNK, p * 2 * dh:(p + 1) * 2 * dh]
            rhs = jnp.concatenate([jnp.where(low, vp, zero), jnp.where(low, zero, vp)], axis=0)
            cols.append(jnp.dot(ws_ref[p], rhs, preferred_element_type=F32))
        chunks.append(jnp.concatenate(cols, axis=-1) + bs_ref[...])
    bm = u * jnp.concatenate(chunks, axis=0)

    cat = jnp.concatenate([a, bm], axis=-1).astype(BF16)
    mix = jnp.dot(cat, wout_ref[...], preferred_element_type=F32)
    o_ref[0] = x_ref[0] + mods_ref[0][5:6] * mix


def _merge(o_f, o_b, zr, x, mods, gla_g, cm_g, ws_pairs, bs_full, wout, tm=512):
    b, t, d = x.shape
    tile = lambda w: pl.BlockSpec((1, tm, w), lambda i, j: (i, j, 0))
    const2 = lambda i, j: (0, 0)
    return pl.pallas_call(
        functools.partial(_merge_kernel, tm=tm),
        grid=(b, t // tm),
        in_specs=[
            tile(GLA_VAL), tile(GLA_VAL), tile(REST_W), tile(d),
            pl.BlockSpec((1, ADA_MODS, d), lambda i, j: (i, 0, 0)),
            pl.BlockSpec((1, GLA_DV), const2),
            pl.BlockSpec((1, CMLP_WIDTH), const2),
            pl.BlockSpec(ws_pairs.shape, lambda i, j: (0, 0, 0)),
            pl.BlockSpec(bs_full.shape, const2),
            pl.BlockSpec(wout.shape, const2),
        ],
        out_specs=tile(d),
        out_shape=jax.ShapeDtypeStruct((b, t, d), F32),
        compiler_params=_cparams("parallel", "arbitrary"),
        name="merge",
    )(o_f, o_b, zr, x, mods, gla_g.reshape(1, GLA_DV), cm_g.reshape(1, CMLP_WIDTH), ws_pairs, bs_full, wout)


def kernel(x, c, ctx, c_ctx, w_ada, b_ada, norm1_g, ff1_in, ff1_out, norm2_g, w_in, conv_w, w_gate_f,
           b_gate_f, w_gate_b, b_gate_b, gla_norm_g, cmlp_norm_g, w_s, b_s, w_out, norm3_g, ff2_in,
           ff2_out, final_g):
    bsz, seq, d = x.shape
    l = 0

    n_cond = bsz + 1
    pad = (-n_cond) % 8
    cond = jnp.concatenate([c, c_ctx[None], jnp.zeros((pad, d), F32)], axis=0)
    mods = _adaln(cond, w_ada[l], b_ada[l]).reshape(n_cond + pad, ADA_MODS, d)
    m_x = mods[:bsz]
    m_c = mods[bsz:bsz + 1]

    ff1_win, ff1_wout = _prep_ffn(ff1_in[l], ff1_out[l])
    ff2_win, ff2_wout = _prep_ffn(ff2_in[l], ff2_out[l])
    wi = w_in[l]
    wq = wi[:, :QKV_W].astype(BF16)
    wl = jnp.pad(wi[:, QKV_W:QKV_W + 2 * GATE_RANK], ((0, 0), (0, GLR_PAD - 2 * GATE_RANK))).astype(BF16)
    wr = wi[:, QKV_W + 2 * GATE_RANK:].astype(BF16)
    conv_w9 = conv_w[l].reshape(9, QKV_W)
    wgate = jnp.zeros((GLR_PAD, 2 * GLA_KEY), F32)
    wgate = wgate.at[:GATE_RANK, :GLA_KEY].set(w_gate_f[l])
    wgate = wgate.at[GATE_RANK:2 * GATE_RANK, GLA_KEY:].set(w_gate_b[l])
    bgate = jnp.concatenate([b_gate_f[l], b_gate_b[l]]).reshape(1, 2 * GLA_KEY)
    ws = w_s[l]
    ws_pairs = jnp.concatenate([ws[0::2], ws[1::2]], axis=-1).astype(BF16)
    bs_full = jnp.repeat(b_s[l].T, CMLP_WIDTH // CMLP_HEADS, axis=1)
    wout = w_out[l].astype(BF16)

    h = ctx.reshape(1, bsz * CTX_LEN, d)
    h = _ffn(h, m_c, norm1_g[l], ff1_win, ff1_wout, (0, 1, 2))
    zq_c, zl_c = _inproj(h, m_c, norm2_g[l], wq, wl, wr, need_rest=False)
    qkv_c, bf_c, bb_c = _conv_gates(zq_c.reshape(bsz, CTX_LEN, QKV_W), zl_c.reshape(bsz, CTX_LEN, GLR_PAD),
                                    conv_w9, wgate, bgate, grid_w=CTX_LEN, tm=CTX_LEN)
    s0 = jnp.zeros((bsz, 2, GLA_DV, GLA_KEY), F32)
    s_ctx = _gla(qkv_c, bf_c, bb_c, s0, emit_o=False)

    x1 = _ffn(x, m_x, norm1_g[l], ff1_win, ff1_wout, (0, 1, 2))
    zq, zl, zr = _inproj(x1, m_x, norm2_g[l], wq, wl, wr, need_rest=True)
    qkv, bf, bb = _conv_gates(zq, zl, conv_w9, wgate, bgate, grid_w=GRID_W, tm=512)
    o_f, o_b = _gla(qkv, bf, bb, s_ctx, emit_o=True)
    x2 = _merge(o_f, o_b, zr, x1, m_x, gla_norm_g[l], cmlp_norm_g[l], ws_pairs, bs_full, wout)
    return _ffn(x2, m_x, norm3_g[l], ff2_win, ff2_wout, (6, 7, 8), final_g=final_g)
```
